```python
import math
import functools
import jax
import jax.numpy as jnp
from jax import lax
import numpy as np

D_MODEL = 2048
BATCH = 2
SEQ = 4096
DEPTH = 2
DEC_BATCH = 8
DEC_SEQ = 4
PAST_LEN = 16384
PAGE_SIZE = 128

N_HEADS = 8
HEAD_DIM = 64
V_DIM = 2 * HEAD_DIM
ATTN_WIDTH = N_HEADS * V_DIM
QK_COLS = N_HEADS * 2 * HEAD_DIM
Q_BLOCK = 128
SUBLN_EPS = 1e-5
SSM_GROUP = 16
SSM_WIDTH = D_MODEL // 2
N_GROUPS = SSM_WIDTH // SSM_GROUP
STATE_DIM = 64
DT_MIN = 0.001
DT_MAX = 0.1
IN_COLS = 2 * QK_COLS + ATTN_WIDTH + SSM_WIDTH + 2 * D_MODEL
N_EXPERTS = 64
TOP_K = 8
N_EXPERT_GROUPS = 8
TOPK_GROUPS = 4
EXPERT_HIDDEN = D_MODEL // 4
SHARED_HIDDEN = D_MODEL // 4
ROUTED_SCALE = 2.5
EPS = 1e-6

kernel_name = 'hybrid_s5_diffattn_moe_adaln_step'

F32 = jnp.float32


def rmsnorm(x, g, eps=EPS):
    x32 = x.astype(F32)
    y = x32 * lax.rsqrt(jnp.mean(x32 * x32, axis=-1, keepdims=True) + eps)
    return (y * g.astype(F32)).astype(x.dtype)


def alibi_slopes():
    return 2.0 ** (-8.0 * jnp.arange(1, N_HEADS + 1, dtype=F32) / N_HEADS)


def diff_attend(q, k, v, tq, tk, lam):
    s = jnp.einsum('qhmd,khmd->hmqk', q.astype(F32), k.astype(F32)) * (HEAD_DIM ** -0.5)
    dist = (tq[:, None] - tk[None, :]).astype(F32)
    bias = -alibi_slopes()[:, None, None, None] * dist
    s = jnp.where(dist >= 0, s + bias, -jnp.inf)
    p = jax.nn.softmax(s, axis=-1)
    a = p[:, 0] - lam * p[:, 1]
    return jnp.einsum('hqk,khe->qhe', a, v.astype(F32))


def prompt_attention(q, k, v, lam):
    b, t = q.shape[0], q.shape[1]
    tk = jnp.arange(t, dtype=jnp.int32)
    n_blocks = t // Q_BLOCK

    def block(i):
        start = i * Q_BLOCK
        qb = lax.dynamic_slice_in_dim(q, start, Q_BLOCK, axis=1)
        tq = start + jnp.arange(Q_BLOCK, dtype=jnp.int32)
        return jax.vmap(diff_attend, in_axes=(0, 0, 0, None, None, None))(qb, k, v, tq, tk, lam)

    out = lax.map(block, jnp.arange(n_blocks, dtype=jnp.int32))
    return out.transpose(1, 0, 2, 3, 4).reshape(b, t, N_HEADS, V_DIM)


def sample_attention(q, k, v, lam, cache_k, cache_v, page_table, layer):
    s_len = q.shape[1]
    n_past = page_table.shape[1] * PAGE_SIZE
    tq = n_past + jnp.arange(s_len, dtype=jnp.int32)
    tk = jnp.arange(n_past + s_len, dtype=jnp.int32)

    def one(args):
        qb, kb, vb, pt = args
        kp = cache_k[layer, pt].reshape(n_past, N_HEADS, 2, HEAD_DIM).astype(kb.dtype)
        vp = cache_v[layer, pt].reshape(n_past, N_HEADS, V_DIM).astype(vb.dtype)
        k_all = jnp.concatenate([kp, kb], axis=0)
        v_all = jnp.concatenate([vp, vb], axis=0)
        return diff_attend(qb, k_all, v_all, tq, tk, lam)

    return lax.map(one, (q, k, v, page_table))


def _complex_affine_combine(earlier, later):
    a1r, a1i, b1r, b1i = earlier
    a2r, a2i, b2r, b2i = later
    return (a2r * a1r - a2i * a1i,
            a2r * a1i + a2i * a1r,
            a2r * b1r - a2i * b1i + b2r,
            a2r * b1i + a2i * b1r + b2i)


def s5_mixer(u, lam_re, lam_im, log_dt, b_re, b_im, c_re, c_im, d_skip, h0_re, h0_im):
    bt, t = u.shape[0], u.shape[1]
    u32 = u.astype(F32).reshape(bt, t, N_GROUPS, SSM_GROUP)
    lr, li = lam_re.astype(F32), lam_im.astype(F32)
    dt = jnp.exp(log_dt.astype(F32))[:, None]
    mag = jnp.exp(lr * dt)
    ar, ai = mag * jnp.cos(li * dt), mag * jnp.sin(li * dt)
    den = lr * lr + li * li
    fr = ((ar - 1.0) * lr + ai * li) / den
    fi = (ai * lr - (ar - 1.0) * li) / den
    br, bi = b_re.astype(F32), b_im.astype(F32)
    bbar_re = fr[..., None] * br - fi[..., None] * bi
    bbar_im = fr[..., None] * bi + fi[..., None] * br
    bu_re = jnp.einsum('btgc,gpc->tbgp', u32, bbar_re)
    bu_im = jnp.einsum('btgc,gpc->tbgp', u32, bbar_im)
    a_re = jnp.broadcast_to(ar, (t, 1, N_GROUPS, STATE_DIM))
    a_im = jnp.broadcast_to(ai, (t, 1, N_GROUPS, STATE_DIM))
    pa_re, pa_im, h_re, h_im = lax.associative_scan(
        _complex_affine_combine, (a_re, a_im, bu_re, bu_im), axis=0)
    if h0_re is not None:
        h0r = h0_re.astype(F32)[None]
        h0i = h0_im.astype(F32)[None]
        h_re, h_im = (h_re + pa_re * h0r - pa_im * h0i,
                      h_im + pa_re * h0i + pa_im * h0r)
    y = (jnp.einsum('tbgp,gcp->btgc', h_re, c_re.astype(F32))
         - jnp.einsum('tbgp,gcp->btgc', h_im, c_im.astype(F32))
         + d_skip.astype(F32).reshape(N_GROUPS, SSM_GROUP) * u32)
    return y.reshape(bt, t, SSM_WIDTH), h_re[-1], h_im[-1]


def swiglu(x, w1, w3, w2):
    return (jax.nn.silu(x @ w1) * (x @ w3)) @ w2


def routed_experts(x, eidx, wts, w1, w3, w2):
    n, d = x.shape
    a = n * TOP_K
    blk = 8
    while blk < 128 and blk * N_EXPERTS < a:
        blk *= 2
    flat_e = eidx.reshape(-1)
    flat_t = jnp.arange(a, dtype=jnp.int32) // TOP_K
    flat_w = wts.reshape(-1)
    order = jnp.argsort(flat_e)
    se, st, sw = flat_e[order], flat_t[order], flat_w[order]
    counts = jnp.bincount(flat_e, length=N_EXPERTS)
    starts = jnp.cumsum(counts) - counts
    padded = (counts + blk - 1) // blk * blk
    pends = jnp.cumsum(padded)
    pstarts = pends - padded
    dest = pstarts[se] + (jnp.arange(a) - starts[se])
    m = (a + N_EXPERTS * (blk - 1) + blk - 1) // blk * blk
    nb = m // blk
    slot_tok = jnp.full((m,), n, jnp.int32).at[dest].set(st)
    slot_w = jnp.zeros((m,), F32).at[dest].set(sw.astype(F32))
    blk_e = jnp.minimum(jnp.searchsorted(pends, jnp.arange(nb) * blk, side='right'), N_EXPERTS - 1)
    x_pad = jnp.concatenate([x, jnp.zeros((1, d), x.dtype)], axis=0)

    def body(acc, inp):
        tok, w, e = inp
        xb = x_pad[tok]
        h = jax.nn.silu(xb @ w1[e]) * (xb @ w3[e])
        yb = (h @ w2[e]).astype(F32) * w[:, None]
        return acc.at[tok].add(yb), None

    acc, _ = lax.scan(body, jnp.zeros((n + 1, d), F32),
                      (slot_tok.reshape(nb, blk), slot_w.reshape(nb, blk), blk_e))
    return acc[:n]


def moe_ffn(z, router_w, router_bias, w1, w3, w2, sw1, sw3, sw2):
    bt, t, d = z.shape
    x = z.reshape(bt * t, d)
    s = jax.nn.sigmoid(x.astype(F32) @ router_w.astype(F32))
    sel = s + router_bias.astype(F32)
    grp = sel.reshape(-1, N_EXPERT_GROUPS, N_EXPERTS // N_EXPERT_GROUPS)
    gscore = lax.top_k(grp, 2)[0].sum(-1)
    _, gidx = lax.top_k(gscore, TOPK_GROUPS)
    gmask = jax.nn.one_hot(gidx, N_EXPERT_GROUPS, dtype=F32).sum(1) > 0
    emask = jnp.repeat(gmask, N_EXPERTS // N_EXPERT_GROUPS, axis=1)
    _, eidx = lax.top_k(jnp.where(emask, sel, -jnp.inf), TOP_K)
    wts = jnp.take_along_axis(s, eidx, axis=1)
    wts = wts / jnp.sum(wts, axis=-1, keepdims=True) * ROUTED_SCALE
    y = routed_experts(x, eidx, wts, w1, w3, w2) + swiglu(x, sw1, sw3, sw2).astype(F32)
    return y.reshape(bt, t, d).astype(z.dtype)


def trunk_layer(x, c, lp, attend, lam_init, h0_re, h0_im):
    bt, t = x.shape[0], x.shape[1]
    mod = jax.nn.silu(c) @ lp['w_ada'] + lp['b_ada']
    sh1, sc1, g1, sh2, sc2, g2 = jnp.split(mod[:, None, :], 6, axis=-1)
    z = rmsnorm(x, lp['norm1_g']) * (1.0 + sc1) + sh1
    proj = z @ lp['w_in']
    cuts = [QK_COLS, 2 * QK_COLS, 2 * QK_COLS + ATTN_WIDTH,
            2 * QK_COLS + ATTN_WIDTH + SSM_WIDTH,
            2 * QK_COLS + ATTN_WIDTH + SSM_WIDTH + D_MODEL]
    q, k, v, u, ga, gs = jnp.split(proj, cuts, axis=-1)
    q = q.reshape(bt, t, N_HEADS, 2, HEAD_DIM)
    k = k.reshape(bt, t, N_HEADS, 2, HEAD_DIM)
    v = v.reshape(bt, t, N_HEADS, V_DIM)
    lam = (jnp.exp(jnp.sum(lp['lambda_q1'].astype(F32) * lp['lambda_k1'].astype(F32)))
           - jnp.exp(jnp.sum(lp['lambda_q2'].astype(F32) * lp['lambda_k2'].astype(F32)))
           + lam_init)
    att = attend(q, k, v, lam)
    att = rmsnorm(att, lp['subln_g'], SUBLN_EPS) * (1.0 - lam_init)
    ya = att.reshape(bt, t, ATTN_WIDTH).astype(x.dtype) @ lp['w_pa']
    ys, hr, hi = s5_mixer(u, lp['ssm_lam_re'], lp['ssm_lam_im'], lp['ssm_log_dt'],
                          lp['ssm_b_re'], lp['ssm_b_im'], lp['ssm_c_re'], lp['ssm_c_im'],
                          lp['ssm_d'], h0_re, h0_im)
    ys = jax.nn.gelu(ys)
    ys = ys * jax.nn.sigmoid(ys @ lp['w_glu'].astype(F32))
    ys = ys.astype(x.dtype) @ lp['w_ps']
    mixed = (jax.nn.sigmoid(ga) * ya + jax.nn.sigmoid(gs) * ys) @ lp['w_out']
    x = x + g1 * mixed
    z2 = rmsnorm(x, lp['norm2_g']) * (1.0 + sc2) + sh2
    x = x + g2 * moe_ffn(z2, lp['router_w'], lp['router_bias'], lp['expert_w1'], lp['expert_w3'],
                         lp['expert_w2'], lp['shared_w1'], lp['shared_w3'], lp['shared_w2'])
    return x, k.reshape(bt, t, N_HEADS, 2 * HEAD_DIM), v, hr, hi


def setup_inputs(seed: int = 0) -> dict:
    key = jax.random.key(seed)
    ks = iter(jax.random.split(key, 48))
    nrm = lambda shape, scale: jax.random.normal(next(ks), shape, F32) * scale
    n_pages = PAST_LEN // PAGE_SIZE
    n_used = DEC_BATCH * n_pages
    n_pool = n_used + max(1, n_used // 4)
    page_table = jax.random.permutation(next(ks), n_pool)[:n_used].reshape(DEC_BATCH, n_pages).astype(jnp.int32)
    lam_im = (jnp.pi * jnp.arange(STATE_DIM, dtype=F32))[None, None, :] + nrm((DEPTH, N_GROUPS, STATE_DIM), 0.01)
    return {
        'x_prompt': nrm((BATCH, SEQ, D_MODEL), 1.0),
        'x_sample': nrm((DEC_BATCH, DEC_SEQ, D_MODEL), 1.0),
        'cache_k': nrm((DEPTH, n_pool, PAGE_SIZE, N_HEADS, 2 * HEAD_DIM), 1.0),
        'cache_v': nrm((DEPTH, n_pool, PAGE_SIZE, N_HEADS, V_DIM), 1.0),
        'state_ssm_re': nrm((DEPTH, DEC_BATCH, N_GROUPS, STATE_DIM), 0.3),
        'state_ssm_im': nrm((DEPTH, DEC_BATCH, N_GROUPS, STATE_DIM), 0.3),
        'page_table': page_table,
        'c_prompt': nrm((BATCH, D_MODEL), 1.0),
        'c_sample': nrm((DEC_BATCH, D_MODEL), 1.0),
        'w_ada': nrm((DEPTH, D_MODEL, 6 * D_MODEL), 0.2 * D_MODEL ** -0.5),
        'b_ada': nrm((DEPTH, 6 * D_MODEL), 0.02),
        'norm1_g': 1.0 + nrm((DEPTH, D_MODEL), 0.02),
        'w_in': nrm((DEPTH, D_MODEL, IN_COLS), D_MODEL ** -0.5),
        'lambda_q1': nrm((DEPTH, HEAD_DIM), 0.1),
        'lambda_k1': nrm((DEPTH, HEAD_DIM), 0.1),
        'lambda_q2': nrm((DEPTH, HEAD_DIM), 0.1),
        'lambda_k2': nrm((DEPTH, HEAD_DIM), 0.1),
        'subln_g': 1.0 + nrm((DEPTH, V_DIM), 0.02),
        'w_pa': nrm((DEPTH, ATTN_WIDTH, D_MODEL), ATTN_WIDTH ** -0.5),
        'ssm_lam_re': -0.5 + nrm((DEPTH, N_GROUPS, STATE_DIM), 0.01),
        'ssm_lam_im': lam_im,
        'ssm_log_dt': jax.random.uniform(next(ks), (DEPTH, N_GROUPS), F32, math.log(DT_MIN), math.log(DT_MAX)),
        'ssm_b_re': nrm((DEPTH, N_GROUPS, STATE_DIM, SSM_GROUP), (2 * SSM_GROUP) ** -0.5),
        'ssm_b_im': nrm((DEPTH, N_GROUPS, STATE_DIM, SSM_GROUP), (2 * SSM_GROUP) ** -0.5),
        'ssm_c_re': nrm((DEPTH, N_GROUPS, SSM_GROUP, STATE_DIM), STATE_DIM ** -0.5),
        'ssm_c_im': nrm((DEPTH, N_GROUPS, SSM_GROUP, STATE_DIM), STATE_DIM ** -0.5),
        'ssm_d': nrm((DEPTH, SSM_WIDTH), 1.0),
        'w_glu': nrm((DEPTH, SSM_WIDTH, SSM_WIDTH), SSM_WIDTH ** -0.5),
        'w_ps': nrm((DEPTH, SSM_WIDTH, D_MODEL), SSM_WIDTH ** -0.5),
        'w_out': nrm((DEPTH, D_MODEL, D_MODEL), D_MODEL ** -0.5),
        'norm2_g': 1.0 + nrm((DEPTH, D_MODEL), 0.02),
        'router_w': nrm((DEPTH, D_MODEL, N_EXPERTS), D_MODEL ** -0.5),
        'router_bias': nrm((DEPTH, N_EXPERTS), 0.01),
        'expert_w1': nrm((DEPTH, N_EXPERTS, D_MODEL, EXPERT_HIDDEN), D_MODEL ** -0.5),
        'expert_w3': nrm((DEPTH, N_EXPERTS, D_MODEL, EXPERT_HIDDEN), D_MODEL ** -0.5),
        'expert_w2': nrm((DEPTH, N_EXPERTS, EXPERT_HIDDEN, D_MODEL), EXPERT_HIDDEN ** -0.5),
        'shared_w1': nrm((DEPTH, D_MODEL, SHARED_HIDDEN), D_MODEL ** -0.5),
        'shared_w3': nrm((DEPTH, D_MODEL, SHARED_HIDDEN), D_MODEL ** -0.5),
        'shared_w2': nrm((DEPTH, SHARED_HIDDEN, D_MODEL), SHARED_HIDDEN ** -0.5),
        'final_g': 1.0 + nrm((D_MODEL,), 0.02),
    }


def reference(x_prompt, x_sample, cache_k, cache_v, state_ssm_re, state_ssm_im, page_table,
              c_prompt, c_sample, w_ada, b_ada, norm1_g, w_in, lambda_q1, lambda_k1, lambda_q2,
              lambda_k2, subln_g, w_pa, ssm_lam_re, ssm_lam_im, ssm_log_dt, ssm_b_re, ssm_b_im,
              ssm_c_re, ssm_c_im, ssm_d, w_glu, w_ps, w_out, norm2_g, router_w, router_bias,
              expert_w1, expert_w3, expert_w2, shared_w1, shared_w3, shared_w2, final_g):
    xp, xs = x_prompt, x_sample
    kp_rows, vp_rows, hrp_all, hip_all = [], [], [], []
    ks_rows, vs_rows, hrs_all, his_all = [], [], [], []
    for l in range(DEPTH):
        lp = dict(w_ada=w_ada[l], b_ada=b_ada[l], norm1_g=norm1_g[l], w_in=w_in[l],
                  lambda_q1=lambda_q1[l], lambda_k1=lambda_k1[l], lambda_q2=lambda_q2[l],
                  lambda_k2=lambda_k2[l], subln_g=subln_g[l], w_pa=w_pa[l],
                  ssm_lam_re=ssm_lam_re[l], ssm_lam_im=ssm_lam_im[l], ssm_log_dt=ssm_log_dt[l],
                  ssm_b_re=ssm_b_re[l], ssm_b_im=ssm_b_im[l], ssm_c_re=ssm_c_re[l],
                  ssm_c_im=ssm_c_im[l], ssm_d=ssm_d[l], w_glu=w_glu[l], w_ps=w_ps[l],
                  w_out=w_out[l], norm2_g=norm2_g[l], router_w=router_w[l],
                  router_bias=router_bias[l], expert_w1=expert_w1[l], expert_w3=expert_w3[l],
                  expert_w2=expert_w2[l], shared_w1=shared_w1[l], shared_w3=shared_w3[l],
                  shared_w2=shared_w2[l])
        lam_init = 0.8 - 0.6 * math.exp(-0.3 * l)
        xp, kp, vp, hrp, hip = trunk_layer(xp, c_prompt, lp, prompt_attention, lam_init, None, None)
        attend_s = functools.partial(sample_attention, cache_k=cache_k, cache_v=cache_v,
                                     page_table=page_table, layer=l)
        xs, ks_, vs_, hrs, his = trunk_layer(xs, c_sample, lp, attend_s, lam_init,
                                             state_ssm_re[l], state_ssm_im[l])
        kp_rows.append(kp); vp_rows.append(vp); hrp_all.append(hrp); hip_all.append(hip)
        ks_rows.append(ks_); vs_rows.append(vs_); hrs_all.append(hrs); his_all.append(his)
    y_prompt = rmsnorm(xp, final_g)
    y_sample = rmsnorm(xs, final_g)
    k_prompt = jnp.stack(kp_rows)
    v_prompt = jnp.stack(vp_rows)
    ssm_re_prompt = jnp.stack(hrp_all)
    ssm_im_prompt = jnp.stack(hip_all)
    k_sample = jnp.stack(ks_rows)
    v_sample = jnp.stack(vs_rows)
    ssm_re_sample = jnp.stack(hrs_all)
    ssm_im_sample = jnp.stack(his_all)
    return (y_prompt, y_sample, k_prompt, v_prompt, ssm_re_prompt, ssm_im_prompt,
            k_sample, v_sample, ssm_re_sample, ssm_im_sample)
```

```python
import functools
import math

import numpy as np
import jax
import jax.numpy as jnp
from jax import lax
from jax.experimental import pallas as pl
from jax.experimental.pallas import tpu as pltpu

F32 = jnp.float32
BF16 = jnp.bfloat16
HIGHEST = lax.Precision.HIGHEST

N_HEADS = 8
HEAD_DIM = 64
V_DIM = 2 * HEAD_DIM
QK_COLS = N_HEADS * 2 * HEAD_DIM
ATTN_WIDTH = N_HEADS * V_DIM
SUBLN_EPS = 1e-5
SSM_GROUP = 16
STATE_DIM = 64
N_EXPERTS = 64
TOP_K = 8
N_EXPERT_GROUPS = 8
TOPK_GROUPS = 4
ROUTED_SCALE = 2.5
EPS = 1e-6
PAGE_SIZE = 128

LANES = 128
SUBLANES = 8
VMEM_LIMIT = 56 * 1024 * 1024

ATTN_TILE = 256
S5_CHUNK = 64
MOE_BLOCK = 256
COMBINE_TILE = 128
PROLOGUE_ROWS = 256


def _cparams(sem):
    return pltpu.CompilerParams(dimension_semantics=sem, vmem_limit_bytes=VMEM_LIMIT)


def _fused_mm(row_ins, prologue, weights, epi_ins, epilogue, *, n_rows, tm, k_dim, n_cols, tn,
              emit_z=False, hi=False, name):
    nr, nw, ne = len(row_ins), len(weights), len(epi_ins)
    zdt = F32 if hi else BF16
    sub = min(tm, PROLOGUE_ROWS)
    assert tm % sub == 0

    def body(*refs):
        row_refs = refs[:nr]
        w_refs = refs[nr:nr + nw]
        epi_refs = refs[nr + nw:nr + nw + ne]
        out_ref = refs[nr + nw + ne]
        z_out = refs[nr + nw + ne + 1] if emit_z else None
        z_ref = refs[-1]

        @pl.when(pl.program_id(1) == 0)
        def _():
            for r0 in range(0, tm, sub):
                rs = pl.ds(r0, sub)
                z = prologue(rs, *row_refs)
                z_ref[rs, :] = z.astype(zdt)
                if emit_z:
                    z_out[rs, :] = z

        z = z_ref[...]
        accs = [jnp.dot(z, w[...].astype(zdt), preferred_element_type=F32,
                        precision=HIGHEST if hi else None) for w in w_refs]
        out_ref[...] = epilogue(accs, *epi_refs).astype(out_ref.dtype)

    in_specs = [pl.BlockSpec(bs, (lambda i, j, im=im: im(i))) for (_, bs, im) in row_ins]
    in_specs += [pl.BlockSpec((k_dim, tn), (lambda i, j, off=off: (0, off + j))) for (_, off) in weights]
    in_specs += [pl.BlockSpec(bs, im) for (_, bs, im) in epi_ins]
    out_shape = [jax.ShapeDtypeStruct((n_rows, n_cols), F32)]
    out_specs = [pl.BlockSpec((tm, tn), lambda i, j: (i, j))]
    if emit_z:
        out_shape.append(jax.ShapeDtypeStruct((n_rows, k_dim), F32))
        out_specs.append(pl.BlockSpec((tm, k_dim), lambda i, j: (i, 0)))
    res = pl.pallas_call(
        body,
        grid=(n_rows // tm, n_cols // tn),
        in_specs=in_specs,
        out_specs=out_specs,
        out_shape=out_shape,
        scratch_shapes=[pltpu.VMEM((tm, k_dim), zdt)],
        compiler_params=_cparams(("parallel", "arbitrary")),
        name=name,
    )(*[a for (a, _, _) in row_ins], *[a for (a, _) in weights], *[a for (a, _, _) in epi_ins])
    return res if emit_z else res[0]


def _rms(x, g, eps):
    return x * lax.rsqrt(jnp.mean(x * x, axis=-1, keepdims=True) + eps) * g


def _norm_mod(rs, x_ref, g_ref, sc_ref, sh_ref):
    return _rms(x_ref[rs, :], g_ref[...], EPS) * (1.0 + sc_ref[0]) + sh_ref[0]


def _first(accs, *_):
    return accs[0]


def _ident(rs, ref):
    return ref[rs, :]


class _Group:
    def __init__(self, n_rows, tm, rows_per_batch, mod, d_model):
        self.n, self.tm, self.rpb, self.mod, self.d = n_rows, tm, rows_per_batch, mod, d_model
        self.rm = mod.shape[1]

    def mod_row(self, chunk):
        tm, rpb = self.tm, self.rpb
        return (self.mod, (1, self.rm, self.d), lambda i: ((i * tm) // rpb, 0, chunk))

    def mod_epi(self, chunk, tn):
        tm, rpb, per = self.tm, self.rpb, self.d // tn
        return (self.mod, (1, self.rm, tn), lambda i, j: ((i * tm) // rpb, 0, chunk * per + j))

    def rows(self, arr, width, col_block=0):
        return (arr, (self.tm, width), lambda i: (i, col_block))


def _lam_value(lam_ref, lam_init):
    lv = lam_ref[...]
    a = jnp.exp(jnp.sum(lv[0:1] * lv[1:2], axis=1, keepdims=True))
    b = jnp.exp(jnp.sum(lv[2:3] * lv[3:4], axis=1, keepdims=True))
    return a - b + lam_init


def _attn_prompt_kernel(lam_ref, slope_ref, g_ref, q_ref, k_ref, v_ref, o_ref, kb, vb, *, t, lam_init):
    qi = pl.program_id(2)

    @pl.when(qi == 0)
    def _():
        kb[...] = k_ref[...].astype(BF16)
        vb[...] = v_ref[...].astype(BF16)

    q = q_ref[...] * (HEAD_DIM ** -0.5)
    lane = lax.broadcasted_iota(jnp.int32, (t, 2 * HEAD_DIM), 1)
    qs = jnp.concatenate([jnp.where(lane < HEAD_DIM, q, 0.0), jnp.where(lane >= HEAD_DIM, q, 0.0)],
                         axis=0).astype(BF16)
    slope = slope_ref[0]
    r = lax.broadcasted_iota(jnp.int32, (2 * t, t), 0)
    c = lax.broadcasted_iota(jnp.int32, (2 * t, t), 1)
    rel = (jnp.where(r >= t, r - t, r) - c).astype(F32)
    relbias = -slope * rel

    def step(j, carry, diag):
        m, l, acc = carry
        start = pl.multiple_of(j * t, t)
        k = kb[pl.ds(start, t), :]
        v = vb[pl.ds(start, t), :]
        s = lax.dot_general(qs, k, (((1,), (1,)), ((), ())), preferred_element_type=F32)
        x = s + relbias
        if diag:
            x = jnp.where(rel >= 0.0, x, -jnp.inf)
        cj = -slope[:, :1] * ((qi - j) * t).astype(F32)
        m_new = jnp.maximum(m, jnp.max(x, axis=1, keepdims=True) + cj)
        p = jnp.exp(x - (m_new - cj))
        alpha = jnp.exp(m - m_new)
        l = alpha * l + jnp.sum(p, axis=1, keepdims=True)
        acc = alpha * acc + jnp.dot(p.astype(BF16), v, preferred_element_type=F32)
        return m_new, l, acc

    init = (jnp.full((2 * t, 1), -jnp.inf, F32), jnp.zeros((2 * t, 1), F32), jnp.zeros((2 * t, V_DIM), F32))
    carry = lax.fori_loop(0, qi, lambda j, cr: step(j, cr, False), init)
    _, l, acc = step(qi, carry, True)
    on = acc / l
    att = on[:t] - _lam_value(lam_ref, lam_init) * on[t:]
    o_ref[...] = _rms(att, g_ref[...], SUBLN_EPS) * (1.0 - lam_init)


def _attn_prompt(proj, lam_vecs, subln_g, batch, seq, lam_init):
    t = min(ATTN_TILE, seq)
    nq = seq // t
    slopes = jnp.asarray(np.broadcast_to(
        (2.0 ** (-np.arange(1, N_HEADS + 1, dtype=np.float64))).astype(np.float32)[:, None, None],
        (N_HEADS, 1, t)))
    kcol, vcol = QK_COLS // V_DIM, 2 * QK_COLS // V_DIM
    return pl.pallas_call(
        functools.partial(_attn_prompt_kernel, t=t, lam_init=lam_init),
        grid=(batch, N_HEADS, nq),
        in_specs=[
            pl.BlockSpec((4, HEAD_DIM), lambda b, h, i: (0, 0)),
            pl.BlockSpec((1, 1, t), lambda b, h, i: (h, 0, 0)),
            pl.BlockSpec((1, V_DIM), lambda b, h, i: (0, 0)),
            pl.BlockSpec((t, V_DIM), lambda b, h, i: (b * nq + i, h)),
            pl.BlockSpec((seq, V_DIM), lambda b, h, i: (b, kcol + h)),
            pl.BlockSpec((seq, V_DIM), lambda b, h, i: (b, vcol + h)),
        ],
        out_specs=pl.BlockSpec((t, V_DIM), lambda b, h, i: (b * nq + i, h)),
        out_shape=jax.ShapeDtypeStruct((batch * seq, ATTN_WIDTH), F32),
        scratch_shapes=[pltpu.VMEM((seq, V_DIM), BF16), pltpu.VMEM((seq, V_DIM), BF16)],
        compiler_params=_cparams(("parallel", "parallel", "arbitrary")),
        name="attn_prompt",
    )(lam_vecs, slopes, subln_g.reshape(1, V_DIM), proj, proj, proj)


def _attn_sample_kernel(pt_ref, lam_ref, slope_ref, g_ref, q_ref, kc_ref, vc_ref, kn_ref, vn_ref,
                        o_ref, m_sc, l_sc, acc_sc, qbd_sc, *, n_pages, s_len, lam_init):
    p = pl.program_id(1)
    rows = N_HEADS * 2 * s_len
    width = N_HEADS * V_DIM

    @pl.when(p == 0)
    def _():
        m_sc[...] = jnp.full_like(m_sc, -jnp.inf)
        l_sc[...] = jnp.zeros_like(l_sc)
        acc_sc[...] = jnp.zeros_like(acc_sc)
        q2 = q_ref[0] * (HEAD_DIM ** -0.5)
        lane = lax.broadcasted_iota(jnp.int32, (2 * s_len, width), 1)
        row = lax.broadcasted_iota(jnp.int32, (2 * s_len, width), 0)
        pieces = []
        for h in range(N_HEADS):
            lo = h * V_DIM + (row // s_len) * HEAD_DIM
            pieces.append(jnp.where((lane >= lo) & (lane < lo + HEAD_DIM), q2, 0.0))
        qbd_sc[...] = jnp.concatenate(pieces, axis=0).astype(BF16)

    def step(kt, vt):
        s = lax.dot_general(qbd_sc[...], kt.astype(BF16), (((1,), (1,)), ((), ())),
                            preferred_element_type=F32)
        r = lax.broadcasted_iota(jnp.int32, (rows, PAGE_SIZE), 0)
        c = lax.broadcasted_iota(jnp.int32, (rows, PAGE_SIZE), 1)
        dist = (n_pages * PAGE_SIZE + r % s_len) - (p * PAGE_SIZE + c)
        x = jnp.where(dist >= 0, s - slope_ref[...] * dist.astype(F32), -jnp.inf)
        m = m_sc[...]
        m_new = jnp.maximum(m, jnp.max(x, axis=1, keepdims=True))
        pr = jnp.exp(x - m_new)
        alpha = jnp.exp(m - m_new)
        l_sc[...] = alpha * l_sc[...] + jnp.sum(pr, axis=1, keepdims=True)
        pv = jnp.dot(pr.astype(BF16), vt.astype(BF16), preferred_element_type=F32)
        g = 2 * s_len
        sel = jnp.concatenate([pv[h * g:(h + 1) * g, h * V_DIM:(h + 1) * V_DIM] for h in range(N_HEADS)], axis=0)
        acc_sc[...] = alpha * acc_sc[...] + sel
        m_sc[...] = m_new

    @pl.when(p < n_pages)
    def _():
        step(kc_ref[0, 0], vc_ref[0, 0])

    @pl.when(p == n_pages)
    def _():
        step(kn_ref[0], vn_ref[0])
        on = acc_sc[...] / l_sc[...]
        on2 = pltpu.roll(on, rows - s_len, axis=0)
        att = on - _lam_value(lam_ref, lam_init) * on2
        o_ref[0] = _rms(att, g_ref[...], SUBLN_EPS) * (1.0 - lam_init)


def _attn_sample(proj, cache_k, cache_v, page_table, layer, lam_vecs, subln_g, batch, s_len, lam_init):
    n_pages = page_table.shape[1]
    width = N_HEADS * V_DIM
    rows = N_HEADS * 2 * s_len
    depth, n_pool = cache_k.shape[0], cache_k.shape[1]
    ck = cache_k.reshape(depth, n_pool, PAGE_SIZE, width)
    cv = cache_v.reshape(depth, n_pool, PAGE_SIZE, width)
    q = proj[:, :QK_COLS].reshape(batch, s_len, width)
    q2 = jnp.concatenate([q, q], axis=1)
    pad = ((0, 0), (0, PAGE_SIZE - s_len), (0, 0))
    kn = jnp.pad(proj[:, QK_COLS:2 * QK_COLS].reshape(batch, s_len, width), pad)
    vn = jnp.pad(proj[:, 2 * QK_COLS:2 * QK_COLS + ATTN_WIDTH].reshape(batch, s_len, width), pad)
    slopes = jnp.asarray(np.broadcast_to(
        np.repeat((2.0 ** (-np.arange(1, N_HEADS + 1, dtype=np.float64))).astype(np.float32), 2 * s_len)[:, None],
        (rows, PAGE_SIZE)))

    def page(b, p, pt):
        return (layer, pt[b, jnp.minimum(p, n_pages - 1)], 0, 0)

    out = pl.pallas_call(
        functools.partial(_attn_sample_kernel, n_pages=n_pages, s_len=s_len, lam_init=lam_init),
        grid_spec=pltpu.PrefetchScalarGridSpec(
            num_scalar_prefetch=1,
            grid=(batch, n_pages + 1),
            in_specs=[
                pl.BlockSpec((4, HEAD_DIM), lambda b, p, pt: (0, 0)),
                pl.BlockSpec((rows, PAGE_SIZE), lambda b, p, pt: (0, 0)),
                pl.BlockSpec((1, V_DIM), lambda b, p, pt: (0, 0)),
                pl.BlockSpec((1, 2 * s_len, width), lambda b, p, pt: (b, 0, 0)),
                pl.BlockSpec((1, 1, PAGE_SIZE, width), page),
                pl.BlockSpec((1, 1, PAGE_SIZE, width), page),
                pl.BlockSpec((1, PAGE_SIZE, width), lambda b, p, pt: (b, 0, 0)),
                pl.BlockSpec((1, PAGE_SIZE, width), lambda b, p, pt: (b, 0, 0)),
            ],
            out_specs=pl.BlockSpec((1, rows, V_DIM), lambda b, p, pt: (b, 0, 0)),
            scratch_shapes=[pltpu.VMEM((rows, 1), F32), pltpu.VMEM((rows, 1), F32),
                            pltpu.VMEM((rows, V_DIM), F32), pltpu.VMEM((rows, width), BF16)],
        ),
        out_shape=jax.ShapeDtypeStruct((batch, rows, V_DIM), F32),
        compiler_params=_cparams(("parallel", "arbitrary")),
        name="attn_sample",
    )(page_table, lam_vecs, slopes, subln_g.reshape(1, V_DIM), q2, ck, cv, kn, vn)
    att = out.reshape(batch, N_HEADS, 2, s_len, V_DIM)[:, :, 0]
    return att.transpose(0, 2, 1, 3).reshape(batch * s_len, ATTN_WIDTH)


def _s5_kernel(*refs, chunk, n_chunks, has_h0, hi):
    if has_h0:
        u_ref, kcat_ref, sb_ref, oc_ref, av_ref, d_ref, h0_ref, y_ref, hf_ref, t_sc = refs
    else:
        u_ref, kcat_ref, sb_ref, oc_ref, av_ref, d_ref, y_ref, hf_ref, t_sc = refs
    cdt = F32 if hi else BF16
    prec = HIGHEST if hi else None
    lw = chunk * SSM_GROUP
    two_p = 2 * STATE_DIM

    kc = kcat_ref[0]
    lane = lax.broadcasted_iota(jnp.int32, (SSM_GROUP, lw), 1)
    for s in range(chunk):
        rolled = kc if s == 0 else pltpu.roll(kc, s * SSM_GROUP, axis=1)
        t_sc[s * SSM_GROUP:(s + 1) * SSM_GROUP, :] = jnp.where(lane >= s * SSM_GROUP, rolled, 0.0).astype(cdt)

    u = u_ref[0]
    uc = u.astype(cdt)
    n_rows = u.shape[0]
    s_end = jnp.dot(uc, sb_ref[0].astype(cdt), preferred_element_type=F32, precision=prec)

    def cmul(h, a_re, a_im):
        return h * a_re + pltpu.roll(h, STATE_DIM, axis=1) * a_im

    a_re = av_ref[0, 0:1, :]
    a_im = av_ref[0, 1:2, :]
    if has_h0:
        h_ent = h0_ref[0]
        h_inc = s_end + cmul(h_ent, a_re, a_im)
    else:
        h_inc = s_end
        rowc = lax.broadcasted_iota(jnp.int32, (n_rows, two_p), 0) % n_chunks
        d = 1
        while d < n_chunks:
            sh = jnp.where(rowc >= d, pltpu.roll(h_inc, d, axis=0), 0.0)
            h_inc = h_inc + cmul(sh, a_re, a_im)
            a_re, a_im = a_re * a_re - a_im * a_im, 2.0 * a_re * a_im
            d *= 2
        h_ent = jnp.where(rowc >= 1, pltpu.roll(h_inc, 1, axis=0), 0.0)
    hf_ref[0] = h_inc
    y = jnp.dot(uc, t_sc[...], preferred_element_type=F32, precision=prec)
    y = y + jnp.dot(h_ent.astype(cdt), oc_ref[0].astype(cdt), preferred_element_type=F32, precision=prec)
    y_ref[0] = y + u * d_ref[0]


def _s5_params(lp, chunk, n_valid):
    lr, li = lp['ssm_lam_re'].astype(F32), lp['ssm_lam_im'].astype(F32)
    g, p = lr.shape
    dt = jnp.exp(lp['ssm_log_dt'].astype(F32))[:, None]
    mag = jnp.exp(lr * dt)
    ar, ai = mag * jnp.cos(li * dt), mag * jnp.sin(li * dt)
    den = lr * lr + li * li
    fr = ((ar - 1.0) * lr + ai * li) / den
    fi = (ai * lr - (ar - 1.0) * li) / den
    br, bi = lp['ssm_b_re'].astype(F32), lp['ssm_b_im'].astype(F32)
    bbr = fr[..., None] * br - fi[..., None] * bi
    bbi = fr[..., None] * bi + fi[..., None] * br
    tau = jnp.arange(chunk + 1, dtype=F32)[None, :, None]
    pmag = jnp.exp(tau * (lr * dt)[:, None, :])
    pw_r = pmag * jnp.cos(tau * (li * dt)[:, None, :])
    pw_i = pmag * jnp.sin(tau * (li * dt)[:, None, :])
    cr = lp['ssm_c_re'].astype(F32).transpose(0, 2, 1)
    ci = lp['ssm_c_im'].astype(F32).transpose(0, 2, 1)
    w_r = pw_r[..., None] * cr[:, None] - pw_i[..., None] * ci[:, None]
    w_i = pw_r[..., None] * ci[:, None] + pw_i[..., None] * cr[:, None]
    kcat = (jnp.einsum('gpi,gtpo->gito', bbr, w_r[:, :chunk], precision=HIGHEST)
            - jnp.einsum('gpi,gtpo->gito', bbi, w_i[:, :chunk], precision=HIGHEST))
    kcat = kcat.reshape(g, SSM_GROUP, chunk * SSM_GROUP)
    oc = jnp.concatenate([w_r[:, 1:].transpose(0, 2, 1, 3), -w_i[:, 1:].transpose(0, 2, 1, 3)], axis=1)
    oc = oc.reshape(g, 2 * p, chunk * SSM_GROUP)
    e = n_valid - 1 - np.arange(chunk)
    valid = jnp.asarray((e >= 0).astype(np.float32))[None, :, None, None]
    idx = np.maximum(e, 0)
    er, ei = pw_r[:, idx], pw_i[:, idx]
    bbr_t, bbi_t = bbr.transpose(0, 2, 1)[:, None], bbi.transpose(0, 2, 1)[:, None]
    sb_r = (er[:, :, None] * bbr_t - ei[:, :, None] * bbi_t) * valid
    sb_i = (er[:, :, None] * bbi_t + ei[:, :, None] * bbr_t) * valid
    sb = jnp.concatenate([sb_r, sb_i], axis=-1).reshape(g, chunk * SSM_GROUP, 2 * p)
    nr, ni = pw_r[:, n_valid], pw_i[:, n_valid]
    avec = jnp.stack([jnp.concatenate([nr, nr], -1), jnp.concatenate([-ni, ni], -1)], axis=1)
    dtile = jnp.tile(lp['ssm_d'].astype(F32).reshape(g, 1, SSM_GROUP), (1, 1, chunk))
    return kcat, sb, oc, avec, dtile


def _s5(u, lp, batch, t_len, h0_re, h0_im):
    g = lp['ssm_lam_re'].shape[0]
    has_h0 = h0_re is not None
    if has_h0:
        chunk = max(SUBLANES, t_len)
        assert chunk * SSM_GROUP % LANES == 0
        t_pad, n_chunks = chunk, 1
    else:
        chunk = min(S5_CHUNK, t_len)
        assert t_len % chunk == 0
        t_pad, n_chunks = t_len, t_len // chunk
    lw = chunk * SSM_GROUP
    kcat, sb, oc, avec, dtile = _s5_params(lp, chunk, min(chunk, t_len))
    u4 = u.reshape(batch, t_len, g, SSM_GROUP)
    if t_pad != t_len:
        u4 = jnp.pad(u4, ((0, 0), (0, t_pad - t_len), (0, 0), (0, 0)))
    rows = batch * n_chunks
    ug = u4.reshape(batch, n_chunks, chunk, g, SSM_GROUP).transpose(3, 0, 1, 2, 4).reshape(g, rows, lw)
    rows_pad = -(-rows // 16) * 16
    if rows_pad != rows:
        ug = jnp.pad(ug, ((0, 0), (0, rows_pad - rows), (0, 0)))
    ins = [ug, kcat, sb, oc, avec, dtile]
    two_p = 2 * STATE_DIM
    in_specs = [
        pl.BlockSpec((1, rows_pad, lw), lambda i: (i, 0, 0)),
        pl.BlockSpec((1, SSM_GROUP, lw), lambda i: (i, 0, 0)),
        pl.BlockSpec((1, lw, two_p), lambda i: (i, 0, 0)),
        pl.BlockSpec((1, two_p, lw), lambda i: (i, 0, 0)),
        pl.BlockSpec((1, 2, two_p), lambda i: (i, 0, 0)),
        pl.BlockSpec((1, 1, lw), lambda i: (i, 0, 0)),
    ]
    if has_h0:
        h0 = jnp.concatenate([h0_re, h0_im], axis=-1).transpose(1, 0, 2)
        h0 = jnp.pad(h0, ((0, 0), (0, rows_pad - rows), (0, 0)))
        ins.append(h0)
        in_specs.append(pl.BlockSpec((1, rows_pad, two_p), lambda i: (i, 0, 0)))
    y, hf = pl.pallas_call(
        functools.partial(_s5_kernel, chunk=chunk, n_chunks=n_chunks, has_h0=has_h0, hi=has_h0),
        grid=(g,),
        in_specs=in_specs,
        out_specs=[pl.BlockSpec((1, rows_pad, lw), lambda i: (i, 0, 0)),
                   pl.BlockSpec((1, rows_pad, two_p), lambda i: (i, 0, 0))],
        out_shape=[jax.ShapeDtypeStruct((g, rows_pad, lw), F32),
                   jax.ShapeDtypeStruct((g, rows_pad, two_p), F32)],
        scratch_shapes=[pltpu.VMEM((lw, lw), F32 if has_h0 else BF16)],
        compiler_params=_cparams(("parallel",)),
        name="s5_sample" if has_h0 else "s5_prompt",
    )(*ins)
    y = y[:, :rows].reshape(g, batch, n_chunks, chunk, SSM_GROUP).transpose(1, 2, 3, 0, 4)
    y = y.reshape(batch, t_pad, g * SSM_GROUP)[:, :t_len].reshape(batch * t_len, g * SSM_GROUP)
    hf = hf[:, :rows].reshape(g, batch, n_chunks, two_p)[:, :, -1].transpose(1, 0, 2)
    return y, hf[..., :STATE_DIM], hf[..., STATE_DIM:]


def _route(s, router_bias):
    sel = s + router_bias.astype(F32)
    grp = sel.reshape(-1, N_EXPERT_GROUPS, N_EXPERTS // N_EXPERT_GROUPS)
    gscore = lax.top_k(grp, 2)[0].sum(-1)
    _, gidx = lax.top_k(gscore, TOPK_GROUPS)
    gmask = jax.nn.one_hot(gidx, N_EXPERT_GROUPS, dtype=F32).sum(1) > 0
    emask = jnp.repeat(gmask, N_EXPERTS // N_EXPERT_GROUPS, axis=1)
    _, eidx = lax.top_k(jnp.where(emask, sel, -jnp.inf), TOP_K)
    wts = jnp.take_along_axis(s, eidx, axis=1)
    wts = wts / jnp.sum(wts, axis=-1, keepdims=True) * ROUTED_SCALE
    return eidx, wts


def _dispatch(eidx, n, blk):
    a = n * TOP_K
    flat_e = eidx.reshape(-1).astype(jnp.int32)
    onehot = (flat_e[:, None] == jnp.arange(N_EXPERTS, dtype=jnp.int32)[None, :]).astype(jnp.int32)
    csum = jnp.cumsum(onehot, axis=0)
    rank = jnp.take_along_axis(csum, flat_e[:, None], axis=1)[:, 0] - 1
    counts = csum[-1]
    padded = (counts + blk - 1) // blk * blk
    pends = jnp.cumsum(padded)
    pstarts = pends - padded
    pos = (pstarts[flat_e] + rank).astype(jnp.int32)
    m = (a + N_EXPERTS * (blk - 1) + blk - 1) // blk * blk
    nb = m // blk
    flat_t = jnp.arange(a, dtype=jnp.int32) // TOP_K
    slot_tok = jnp.zeros((m,), jnp.int32).at[pos].set(flat_t)
    blk_e = jnp.minimum(jnp.searchsorted(pends, jnp.arange(nb, dtype=jnp.int32) * blk, side='right'),
                        N_EXPERTS - 1).astype(jnp.int32)
    n_used = (pends[-1] // blk).astype(jnp.int32).reshape(1)
    return slot_tok, pos, blk_e, n_used, m, nb


def _experts_kernel(be_ref, nu_ref, tok_ref, x_hbm, w1_ref, w3_ref, w2_ref, y_ref,
                    xbuf, w1b, w3b, w2b, sem, *, blk):
    j = pl.program_id(0)
    used = j < nu_ref[0]

    def row_copy(r):
        t = tok_ref[j * blk + r]
        return pltpu.make_async_copy(x_hbm.at[pl.ds(t, 1)], xbuf.at[pl.ds(r, 1)], sem)

    @pl.when(used)
    def _():
        def issue(r, carry):
            row_copy(r).start()
            return carry
        lax.fori_loop(0, blk, issue, 0)

    prev = be_ref[jnp.maximum(j - 1, 0)]

    @pl.when(used & ((j == 0) | (be_ref[j] != prev)))
    def _():
        w1b[...] = w1_ref[0].astype(BF16)
        w3b[...] = w3_ref[0].astype(BF16)
        w2b[...] = w2_ref[0].astype(BF16)

    @pl.when(used)
    def _():
        def drain(r, carry):
            row_copy(r).wait()
            return carry
        lax.fori_loop(0, blk, drain, 0)
        xb = xbuf[...].astype(BF16)
        h1 = jnp.dot(xb, w1b[...], preferred_element_type=F32)
        h3 = jnp.dot(xb, w3b[...], preferred_element_type=F32)
        h = (jax.nn.silu(h1) * h3).astype(BF16)
        y_ref[...] = jnp.dot(h, w2b[...], preferred_element_type=F32)

    @pl.when(jnp.logical_not(used))
    def _():
        y_ref[...] = jnp.zeros_like(y_ref)


def _experts(x, slot_tok, blk_e, n_used, w1, w3, w2, m, nb, blk):
    d = x.shape[1]
    hid = w1.shape[2]
    return pl.pallas_call(
        functools.partial(_experts_kernel, blk=blk),
        grid_spec=pltpu.PrefetchScalarGridSpec(
            num_scalar_prefetch=3,
            grid=(nb,),
            in_specs=[
                pl.BlockSpec(memory_space=pl.ANY),
                pl.BlockSpec((1, d, hid), lambda j, be, nu, tok: (be[j], 0, 0)),
                pl.BlockSpec((1, d, hid), lambda j, be, nu, tok: (be[j], 0, 0)),
                pl.BlockSpec((1, hid, d), lambda j, be, nu, tok: (be[j], 0, 0)),
            ],
            out_specs=pl.BlockSpec((blk, d), lambda j, be, nu, tok: (j, 0)),
            scratch_shapes=[pltpu.VMEM((blk, d), F32), pltpu.VMEM((d, hid), BF16), pltpu.VMEM((d, hid), BF16),
                            pltpu.VMEM((hid, d), BF16), pltpu.SemaphoreType.DMA(())],
        ),
        out_shape=jax.ShapeDtypeStruct((m, d), F32),
        compiler_params=_cparams(("arbitrary",)),
        name="moe_experts",
    )(blk_e, n_used, slot_tok, x, w1, w3, w2)


def _combine_kernel(pos_ref, y_hbm, x_ref, hs_ref, sw2_ref, w_ref, g2_ref, fg_ref, o_ref, buf, sem, *, tt, final):
    i = pl.program_id(0)

    def row_copy(r, k):
        p = pos_ref[(i * tt + r) * TOP_K + k]
        return pltpu.make_async_copy(y_hbm.at[pl.ds(p, 1)], buf.at[k, pl.ds(r, 1)], sem)

    def issue(r, carry):
        for k in range(TOP_K):
            row_copy(r, k).start()
        return carry
    lax.fori_loop(0, tt, issue, 0)

    shared = jnp.dot(hs_ref[...].astype(BF16), sw2_ref[...].astype(BF16), preferred_element_type=F32)

    def drain(r, carry):
        for k in range(TOP_K):
            row_copy(r, k).wait()
        return carry
    lax.fori_loop(0, tt, drain, 0)

    w = w_ref[...]
    routed = buf[0] * w[:, 0:1]
    for k in range(1, TOP_K):
        routed = routed + buf[k] * w[:, k:k + 1]
    out = x_ref[...] + g2_ref[0] * (routed + shared)
    if final:
        out = _rms(out, fg_ref[...], EPS)
    o_ref[...] = out


def _combine(y_sorted, pos, x, hs, sw2, wts, grp, final_g, final):
    n, d = x.shape
    tt = min(COMBINE_TILE, n)
    hid = hs.shape[1]
    tm, rpb, rm = tt, grp.rpb, grp.rm
    return pl.pallas_call(
        functools.partial(_combine_kernel, tt=tt, final=final),
        grid_spec=pltpu.PrefetchScalarGridSpec(
            num_scalar_prefetch=1,
            grid=(n // tt,),
            in_specs=[
                pl.BlockSpec(memory_space=pl.ANY),
                pl.BlockSpec((tt, d), lambda i, pos: (i, 0)),
                pl.BlockSpec((tt, hid), lambda i, pos: (i, 0)),
                pl.BlockSpec((hid, d), lambda i, pos: (0, 0)),
                pl.BlockSpec((tt, TOP_K), lambda i, pos: (i, 0)),
                pl.BlockSpec((1, min(rm, tt), d), lambda i, pos: ((i * tm) // rpb, 0, 5)),
                pl.BlockSpec((1, d), lambda i, pos: (0, 0)),
            ],
            out_specs=pl.BlockSpec((tt, d), lambda i, pos: (i, 0)),
            scratch_shapes=[pltpu.VMEM((TOP_K, tt, d), F32), pltpu.SemaphoreType.DMA(())],
        ),
        out_shape=jax.ShapeDtypeStruct((n, d), F32),
        compiler_params=_cparams(("arbitrary",)),
        name="moe_combine",
    )(pos, y_sorted, x, hs, sw2, wts, grp.mod, final_g.reshape(1, d))


def _gelu_tanh(x):
    return 0.5 * x * (1.0 + jnp.tanh(math.sqrt(2.0 / math.pi) * (x + 0.044715 * (x * x * x))))


def _layer(x, grp, lp, lam_init, attend, batch, t_len, h0_re, h0_im, final_g, final):
    n, d = x.shape
    tm = grp.tm
    in_cols = lp['w_in'].shape[1]
    ssm_w = lp['w_glu'].shape[0]
    g1d = lp['norm1_g'].reshape(1, d)
    g2d = lp['norm2_g'].reshape(1, d)
    full = lambda arr: (arr, arr.shape, lambda i: (0,) * arr.ndim)

    proj = _fused_mm([grp.rows(x, d), full(g1d), grp.mod_row(1), grp.mod_row(0)], _norm_mod,
                     [(lp['w_in'], 0)], [], _first,
                     n_rows=n, tm=tm, k_dim=d, n_cols=in_cols, tn=512, name="in_proj")
    k_rows = proj[:, QK_COLS:2 * QK_COLS]
    v_rows = proj[:, 2 * QK_COLS:2 * QK_COLS + ATTN_WIDTH]
    u = proj[:, 2 * QK_COLS + ATTN_WIDTH:2 * QK_COLS + ATTN_WIDTH + ssm_w]
    lam_vecs = jnp.stack([lp['lambda_q1'], lp['lambda_k1'], lp['lambda_q2'], lp['lambda_k2']]).astype(F32)
    att = attend(proj, lam_vecs, lp['subln_g'].astype(F32))
    ya = _fused_mm([grp.rows(att, ATTN_WIDTH)], _ident, [(lp['w_pa'], 0)], [], _first,
                   n_rows=n, tm=tm, k_dim=ATTN_WIDTH, n_cols=d, tn=512, name="attn_out")
    ys_raw, hr, hi_ = _s5(u, lp, batch, t_len, h0_re, h0_im)
    tn_g = 512
    ysg = _fused_mm([grp.rows(ys_raw, ssm_w)], lambda rs, r: _gelu_tanh(r[rs, :]), [(lp['w_glu'], 0)],
                    [(ys_raw, (tm, tn_g), lambda i, j: (i, j))],
                    lambda accs, yr: _gelu_tanh(yr[...]) * jax.nn.sigmoid(accs[0]),
                    n_rows=n, tm=tm, k_dim=ssm_w, n_cols=ssm_w, tn=tn_g, name="ssm_glu")
    ys = _fused_mm([grp.rows(ysg, ssm_w)], _ident, [(lp['w_ps'], 0)], [], _first,
                   n_rows=n, tm=tm, k_dim=ssm_w, n_cols=d, tn=512, name="ssm_out")
    tm_o = min(tm, 256)
    grp_o = _Group(n, tm_o, grp.rpb, grp.mod if grp.rm == 1 else grp.mod, d)
    ga_blk = (2 * QK_COLS + ATTN_WIDTH + ssm_w) // d
    assert ga_blk * d == 2 * QK_COLS + ATTN_WIDTH + ssm_w
    x1 = _fused_mm([grp_o.rows(proj, d, ga_blk), grp_o.rows(proj, d, ga_blk + 1), grp_o.rows(ya, d), grp_o.rows(ys, d)],
                   lambda rs, ga, gs, a, s: jax.nn.sigmoid(ga[rs, :]) * a[rs, :] + jax.nn.sigmoid(gs[rs, :]) * s[rs, :],
                   [(lp['w_out'], 0)],
                   [(x, (tm_o, 512), lambda i, j: (i, j)), grp_o.mod_epi(2, 512)],
                   lambda accs, xr, g1: xr[...] + g1[0] * accs[0],
                   n_rows=n, tm=tm_o, k_dim=d, n_cols=d, tn=512, name="mix_out")
    tm_r = min(tm, 512)
    grp_r = _Group(n, tm_r, grp.rpb, grp.mod, d)
    s_aff, z2 = _fused_mm([grp_r.rows(x1, d), full(g2d), grp_r.mod_row(4), grp_r.mod_row(3)], _norm_mod,
                          [(lp['router_w'], 0)], [], lambda accs: jax.nn.sigmoid(accs[0]),
                          n_rows=n, tm=tm_r, k_dim=d, n_cols=N_EXPERTS, tn=N_EXPERTS,
                          emit_z=True, hi=True, name="router")
    hid = lp['shared_w1'].shape[1]
    hs = _fused_mm([grp.rows(z2, d)], _ident, [(lp['shared_w1'], 0), (lp['shared_w3'], 0)], [],
                   lambda accs: jax.nn.silu(accs[0]) * accs[1],
                   n_rows=n, tm=tm, k_dim=d, n_cols=hid, tn=hid, name="shared_up")
    eidx, wts = _route(s_aff, lp['router_bias'])
    blk = MOE_BLOCK if n * TOP_K >= MOE_BLOCK * N_EXPERTS else 16
    slot_tok, pos, blk_e, n_used, m, nb = _dispatch(eidx, n, blk)
    y_sorted = _experts(z2, slot_tok, blk_e, n_used, lp['expert_w1'], lp['expert_w3'], lp['expert_w2'], m, nb, blk)
    x2 = _combine(y_sorted, pos, x1, hs, lp['shared_w2'], wts, grp, final_g, final)
    return x2, k_rows, v_rows, hr, hi_


def _adaln(c_all, w_ada, b_ada):
    n, d = c_all.shape
    cols = w_ada.shape[1]
    return _fused_mm([(c_all, (n, d), lambda i: (0, 0))], lambda rs, c: jax.nn.silu(c[rs, :]),
                     [(w_ada, 0)], [(b_ada.reshape(1, cols), (1, 1024), lambda i, j: (0, j))],
                     lambda accs, b: accs[0] + b[...],
                     n_rows=n, tm=n, k_dim=d, n_cols=cols, tn=1024, hi=True, name="adaln")


def kernel(x_prompt, x_sample, cache_k, cache_v, state_ssm_re, state_ssm_im, page_table, c_prompt, c_sample, w_ada, b_ada, norm1_g, w_in, lambda_q1, lambda_k1, lambda_q2, lambda_k2, subln_g, w_pa, ssm_lam_re, ssm_lam_im, ssm_log_dt, ssm_b_re, ssm_b_im, ssm_c_re, ssm_c_im, ssm_d, w_glu, w_ps, w_out, norm2_g, router_w, router_bias, expert_w1, expert_w3, expert_w2, shared_w1, shared_w3, shared_w2, final_g):
    batch, seq, d = x_prompt.shape
    dec_b, dec_s, _ = x_sample.shape
    depth = w_in.shape[0]
    n_p, n_s = batch * seq, dec_b * dec_s
    xp = x_prompt.reshape(n_p, d)
    xs = x_sample.reshape(n_s, d)
    c_rows = batch + dec_b
    c_pad = -(-c_rows // 16) * 16
    c_all = jnp.pad(jnp.concatenate([c_prompt, c_sample], axis=0), ((0, c_pad - c_rows), (0, 0)))
    tm_p = min(1024, seq)
    outs = {k: [] for k in ('kp', 'vp', 'hrp', 'hip', 'ks', 'vs', 'hrs', 'his')}
    for l in range(depth):
        lp = dict(norm1_g=norm1_g[l], w_in=w_in[l], lambda_q1=lambda_q1[l], lambda_k1=lambda_k1[l],
                  lambda_q2=lambda_q2[l], lambda_k2=lambda_k2[l], subln_g=subln_g[l], w_pa=w_pa[l],
                  ssm_lam_re=ssm_lam_re[l], ssm_lam_im=ssm_lam_im[l], ssm_log_dt=ssm_log_dt[l],
                  ssm_b_re=ssm_b_re[l], ssm_b_im=ssm_b_im[l], ssm_c_re=ssm_c_re[l], ssm_c_im=ssm_c_im[l],
                  ssm_d=ssm_d[l], w_glu=w_glu[l], w_ps=w_ps[l], w_out=w_out[l], norm2_g=norm2_g[l],
                  router_w=router_w[l], router_bias=router_bias[l], expert_w1=expert_w1[l],
                  expert_w3=expert_w3[l], expert_w2=expert_w2[l], shared_w1=shared_w1[l],
                  shared_w3=shared_w3[l], shared_w2=shared_w2[l])
        lam_init = 0.8 - 0.6 * math.exp(-0.3 * l)
        final = l == depth - 1
        mod = _adaln(c_all, w_ada[l], b_ada[l])
        mod_p = mod[:batch].reshape(batch, 1, 6 * d)
        mod_s = jnp.repeat(mod[batch:c_rows], dec_s, axis=0).reshape(1, n_s, 6 * d)
        grp_p = _Group(n_p, tm_p, seq, mod_p, d)
        grp_s = _Group(n_s, n_s, n_s, mod_s, d)
        attend_p = functools.partial(_attn_prompt, batch=batch, seq=seq, lam_init=lam_init)
        xp, kp, vp, hrp, hip = _layer(xp, grp_p, lp, lam_init, attend_p, batch, seq, None, None, final_g, final)
        attend_s = lambda proj, lam_vecs, sg, l=l, lam_init=lam_init: _attn_sample(
            proj, cache_k, cache_v, page_table, l, lam_vecs, sg, dec_b, dec_s, lam_init)
        xs, ks_, vs_, hrs, his = _layer(xs, grp_s, lp, lam_init, attend_s, dec_b, dec_s,
                                        state_ssm_re[l], state_ssm_im[l], final_g, final)
        outs['kp'].append(kp.reshape(batch, seq, N_HEADS, 2 * HEAD_DIM))
        outs['vp'].append(vp.reshape(batch, seq, N_HEADS, V_DIM))
        outs['hrp'].append(hrp)
        outs['hip'].append(hip)
        outs['ks'].append(ks_.reshape(dec_b, dec_s, N_HEADS, 2 * HEAD_DIM))
        outs['vs'].append(vs_.reshape(dec_b, dec_s, N_HEADS, V_DIM))
        outs['hrs'].append(hrs)
        outs['his'].append(his)
    st = jnp.stack
    return (xp.reshape(batch, seq, d), xs.reshape(dec_b, dec_s, d), st(outs['kp']), st(outs['vp']),
            st(outs['hrp']), st(outs['hip']), st(outs['ks']), st(outs['vs']), st(outs['hrs']), st(outs['his']))
```
